```python
import functools
import jax, jax.numpy as jnp
from jax import lax
import numpy as np

D_MODEL = 1024
BATCH = 32
SEQ = 2048
DEPTH = 1
DEC_BATCH = 128
DEC_SEQ = 1
PAST_LEN = 8192
PAGE_SIZE = 128

D_RNN = 1024
N_LRU_BLOCKS = 4
LRU_BLOCK = D_RNN // N_LRU_BLOCKS
CONV_W = 4
LRU_C = 8.0
HEAD_DIM = 128
HEADS_PER_GROUP = 4
GROUPS = ((128, 1), (512, 4), (2048, 16))
N_GROUPS = len(GROUPS)
N_HEADS = N_GROUPS * HEADS_PER_GROUP
QKV_W = N_HEADS * HEAD_DIM
ATT_OUT = HEADS_PER_GROUP * HEAD_DIM
ROT_DIM = HEAD_DIM // 4
ROPE_THETA = 500000.0
N_EXPERTS = 32
TOP_K = 4
D_FF = 1024
SWIGLU_LIMIT = 7.0
SWIGLU_ALPHA = 1.702
MOE_BLOCK = 128
PLE_DIM = 256
EPS = 1e-6
IN_SIZES = (D_RNN, D_RNN, QKV_W, QKV_W, QKV_W, D_MODEL, D_MODEL)
N_IN = sum(IN_SIZES)

kernel_name = 'hybrid_rglru_dilated_swa_moe_step'


def rmsnorm(x, g):
    xf = x.astype(jnp.float32)
    y = xf * lax.rsqrt(jnp.mean(xf * xf, axis=-1, keepdims=True) + EPS)
    return (y * g.astype(jnp.float32)).astype(x.dtype)


def partial_rope(x, pos):
    half = ROT_DIM // 2
    freqs = jnp.float32(ROPE_THETA) ** (-jnp.arange(half, dtype=jnp.float32) * (2.0 / ROT_DIM))
    ang = pos.astype(jnp.float32)[:, None] * freqs[None, :]
    cos = jnp.cos(ang)[None, :, None, :]
    sin = jnp.sin(ang)[None, :, None, :]
    xf = x.astype(jnp.float32)
    x1 = xf[..., :half]
    x2 = xf[..., half:ROT_DIM]
    out = jnp.concatenate([x1 * cos - x2 * sin, x2 * cos + x1 * sin, xf[..., ROT_DIM:]], axis=-1)
    return out.astype(x.dtype)


def causal_conv(x, buf, w, b):
    T = x.shape[1]
    xp = jnp.concatenate([buf.astype(x.dtype), x], axis=1)
    y = sum(xp[:, k:k + T] * w[k] for k in range(CONV_W)) + b
    return y, xp[:, -(CONV_W - 1):]


def rglru(x, h0, w_a, b_a, w_x, b_x, lam):
    xf = x.astype(jnp.float32)
    xb = xf.reshape(*xf.shape[:-1], N_LRU_BLOCKS, LRU_BLOCK)
    r = jax.nn.sigmoid(jnp.einsum('btnc,ncd->btnd', xb, w_a.astype(jnp.float32)).reshape(xf.shape) + b_a.astype(jnp.float32))
    i = jax.nn.sigmoid(jnp.einsum('btnc,ncd->btnd', xb, w_x.astype(jnp.float32)).reshape(xf.shape) + b_x.astype(jnp.float32))
    log_a = -LRU_C * r * jax.nn.softplus(-lam.astype(jnp.float32))
    a = jnp.exp(log_a)
    u = jnp.sqrt(-jnp.expm1(2.0 * log_a)) * (i * xf)

    def step(h, au):
        a_t, u_t = au
        h = a_t * h + u_t
        return h, h

    h_last, hs = lax.scan(step, h0.astype(jnp.float32), (a.transpose(1, 0, 2), u.transpose(1, 0, 2)))
    return hs.transpose(1, 0, 2).astype(x.dtype), h_last.astype(h0.dtype)


def band_dilated_attention(q, k, v, window, dil):
    B, S, H, D = q.shape
    bl = window // dil
    span = bl * dil
    s_pad = -(-S // span) * span
    nb = s_pad // span

    def blocks(t):
        t = jnp.pad(t, ((0, 0), (0, s_pad - S), (0, 0), (0, 0)))
        t = t.reshape(B, nb * bl, dil, H, D).transpose(0, 2, 1, 3, 4)
        return t.reshape(B, dil, nb, bl, H, D)

    def with_prev(t):
        prev = jnp.pad(t, ((0, 0), (0, 0), (1, 0), (0, 0), (0, 0), (0, 0)))[:, :, :nb]
        return jnp.concatenate([prev, t], axis=3)

    qb = blocks(q).astype(jnp.float32)
    k2 = with_prev(blocks(k)).astype(jnp.float32)
    v2 = with_prev(blocks(v)).astype(jnp.float32)
    s = jnp.einsum('brnqhd,brnkhd->brnhqk', qb, k2) * (HEAD_DIM ** -0.5)
    qi = jnp.arange(bl)[:, None] + bl
    kj = jnp.arange(2 * bl)[None, :]
    rel = qi - kj
    band = (rel >= 0) & (rel <= bl)
    has_prev = (jnp.arange(nb)[:, None, None] > 0) | (kj[None] >= bl)
    mask = band[None] & has_prev
    s = jnp.where(mask[None, None, :, None], s, -jnp.inf)
    lse = jax.nn.logsumexp(s, axis=-1)
    p = jnp.exp(s - lse[..., None])
    o = jnp.einsum('brnhqk,brnkhd->brnqhd', p, v2)
    o = o.reshape(B, dil, nb * bl, H, D).transpose(0, 2, 1, 3, 4).reshape(B, s_pad, H, D)[:, :S]
    lse = lse.transpose(0, 1, 2, 4, 3).reshape(B, dil, nb * bl, H).transpose(0, 2, 1, 3).reshape(B, s_pad, H)[:, :S]
    return o, lse


def gathered_dilated_attention(q, k_all, v_all, n_past, window, dil):
    T = q.shape[1]
    nk = window // dil + 1
    idx = n_past + jnp.arange(T)[:, None] - dil * jnp.arange(nk)[None, :]
    valid = idx >= 0
    idx = jnp.maximum(idx, 0)
    kg = k_all[:, idx].astype(jnp.float32)
    vg = v_all[:, idx].astype(jnp.float32)
    s = jnp.einsum('bthd,btjhd->bthj', q.astype(jnp.float32), kg) * (HEAD_DIM ** -0.5)
    s = jnp.where(valid[None, :, None, :], s, -jnp.inf)
    lse = jax.nn.logsumexp(s, axis=-1)
    p = jnp.exp(s - lse[..., None])
    o = jnp.einsum('bthj,btjhd->bthd', p, vg)
    return o, lse


def combine_groups(outs, lses, dtype):
    o = jnp.stack(outs, axis=0)
    w = jax.nn.softmax(jnp.stack(lses, axis=0), axis=0)
    att = jnp.sum(w[..., None] * o, axis=0)
    B, T = att.shape[:2]
    return att.reshape(B, T, ATT_OUT).astype(dtype)


def attend_prompt(q, k, v):
    S = q.shape[1]
    outs, lses, news = [], [], []
    for g, (window, dil) in enumerate(GROUPS):
        sl = slice(g * HEADS_PER_GROUP, (g + 1) * HEADS_PER_GROUP)
        o, lse = band_dilated_attention(q[:, :, sl], k[:, :, sl], v[:, :, sl], window, dil)
        outs.append(o)
        lses.append(lse)
        wp = min(window, S)
        news.append(jnp.stack([k[:, S - wp:, sl], v[:, S - wp:, sl]], axis=2))
    return combine_groups(outs, lses, q.dtype), news


def attend_sample(q, k, v, caches):
    outs, lses, news = [], [], []
    for g, (window, dil) in enumerate(GROUPS):
        sl = slice(g * HEADS_PER_GROUP, (g + 1) * HEADS_PER_GROUP)
        cache = caches[g]
        wb = cache.shape[1]
        k_all = jnp.concatenate([cache[:, :, 0], k[:, :, sl].astype(cache.dtype)], axis=1)
        v_all = jnp.concatenate([cache[:, :, 1], v[:, :, sl].astype(cache.dtype)], axis=1)
        o, lse = gathered_dilated_attention(q[:, :, sl], k_all, v_all, wb, window, dil)
        outs.append(o)
        lses.append(lse)
        news.append(jnp.stack([k_all[:, -wb:], v_all[:, -wb:]], axis=2))
    return combine_groups(outs, lses, q.dtype), news


def moe(x, w_router, b_router, w_gu, b_gu, w_down, b_down):
    N, D = x.shape
    logits = x.astype(jnp.float32) @ w_router.astype(jnp.float32) + b_router.astype(jnp.float32)
    top_v, top_e = lax.top_k(logits, TOP_K)
    gates = jax.nn.softmax(top_v, axis=-1)
    na = N * TOP_K
    flat_e = top_e.reshape(na)
    order = jnp.argsort(flat_e)
    sorted_e = flat_e[order]
    sorted_tok = order // TOP_K
    counts = jnp.bincount(flat_e, length=N_EXPERTS)
    padded = (counts + MOE_BLOCK - 1) // MOE_BLOCK * MOE_BLOCK
    pad_end = jnp.cumsum(padded)
    pad_start = pad_end - padded
    start = jnp.cumsum(counts) - counts
    dest = pad_start[sorted_e] + jnp.arange(na) - start[sorted_e]
    nb = -(-na // MOE_BLOCK) + N_EXPERTS
    rows = nb * MOE_BLOCK
    row_tok = jnp.full((rows,), N, jnp.int32).at[dest].set(sorted_tok.astype(jnp.int32))
    row_gate = jnp.zeros((rows,), jnp.float32).at[dest].set(gates.reshape(na)[order])
    blk_e = jnp.minimum(jnp.searchsorted(pad_end, jnp.arange(nb) * MOE_BLOCK, side='right'), N_EXPERTS - 1)
    xpad = jnp.concatenate([x, jnp.zeros((1, D), x.dtype)], axis=0)
    xb = xpad[row_tok].reshape(nb, MOE_BLOCK, D)

    def expert_rows(args):
        xblk, e = args
        gu = xblk @ w_gu[e] + b_gu[e]
        gate = jnp.minimum(gu[:, :D_FF], SWIGLU_LIMIT)
        up = jnp.clip(gu[:, D_FF:], -SWIGLU_LIMIT, SWIGLU_LIMIT)
        hmid = (up + 1.0) * (gate * jax.nn.sigmoid(SWIGLU_ALPHA * gate))
        return hmid @ w_down[e] + b_down[e]

    yb = lax.map(expert_rows, (xb, blk_e))
    y = yb.reshape(rows, D) * row_gate[:, None].astype(x.dtype)
    return jnp.zeros((N + 1, D), x.dtype).at[row_tok].add(y)[:N]


def layer(x, ple, conv_buf, h0, pos, attend, lw):
    B, T, _ = x.shape
    h = rmsnorm(x, lw['g_mix'])
    z = h @ lw['w_in']
    xr, gr, q, k, v, g_lru, g_att = jnp.split(z, np.cumsum(IN_SIZES)[:-1].tolist(), axis=-1)
    xc, conv_new = causal_conv(xr, conv_buf, lw['conv_w'], lw['conv_b'])
    yr, h_new = rglru(xc, h0, lw['w_rg_a'], lw['b_rg_a'], lw['w_rg_x'], lw['b_rg_x'], lw['lru_lambda'])
    b_lru = (yr * jax.nn.gelu(gr)) @ lw['w_br_lru']
    q = partial_rope(rmsnorm(q.reshape(B, T, N_HEADS, HEAD_DIM), lw['g_q']), pos)
    k = partial_rope(rmsnorm(k.reshape(B, T, N_HEADS, HEAD_DIM), lw['g_k']), pos)
    v = v.reshape(B, T, N_HEADS, HEAD_DIM)
    att, kv_new = attend(q, k, v)
    b_att = att @ lw['w_br_att']
    merged = jax.nn.sigmoid(g_lru) * b_lru + jax.nn.sigmoid(g_att) * b_att
    x = x + merged @ lw['w_out']
    hf = rmsnorm(x, lw['g_ffn'])
    x = x + moe(hf.reshape(B * T, D_MODEL), lw['w_router'], lw['b_router'], lw['w_gu'], lw['b_gu'], lw['w_down'], lw['b_down']).reshape(B, T, D_MODEL)
    x = x + jax.nn.sigmoid(rmsnorm(x, lw['g_ple']) @ lw['w_ple_gate']) * (ple.astype(x.dtype) @ lw['w_ple'])
    return x, kv_new, conv_new, h_new


def setup_inputs(seed: int = 0) -> dict:
    key = jax.random.key(seed)
    ks = iter(jax.random.split(key, 40))

    def nrm(shape, scale=1.0):
        return jax.random.normal(next(ks), shape, jnp.float32) * scale

    L = DEPTH
    wbs = [min(w, PAST_LEN) for w, _ in GROUPS]
    a0 = jax.random.uniform(next(ks), (L, D_RNN), jnp.float32, 0.9, 0.999)
    s0 = a0 ** (1.0 / LRU_C)
    lru_lambda = jnp.log(s0) - jnp.log1p(-s0)
    return {
        'x_prompt': nrm((BATCH, SEQ, D_MODEL)),
        'x_sample': nrm((DEC_BATCH, DEC_SEQ, D_MODEL)),
        'cache_kv_g0': nrm((L, DEC_BATCH, wbs[0], 2, HEADS_PER_GROUP, HEAD_DIM)),
        'cache_kv_g1': nrm((L, DEC_BATCH, wbs[1], 2, HEADS_PER_GROUP, HEAD_DIM)),
        'cache_kv_g2': nrm((L, DEC_BATCH, wbs[2], 2, HEADS_PER_GROUP, HEAD_DIM)),
        'state_conv': nrm((L, DEC_BATCH, CONV_W - 1, D_RNN)),
        'state_h': nrm((L, DEC_BATCH, D_RNN), 0.5),
        'p_prompt': nrm((L, BATCH, SEQ, PLE_DIM)),
        'p_sample': nrm((L, DEC_BATCH, DEC_SEQ, PLE_DIM)),
        'g_mix': 1.0 + nrm((L, D_MODEL), 0.02),
        'w_in': nrm((L, D_MODEL, N_IN), D_MODEL ** -0.5),
        'conv_w': nrm((L, CONV_W, D_RNN), CONV_W ** -0.5),
        'conv_b': nrm((L, D_RNN), 0.01),
        'w_rg_a': nrm((L, N_LRU_BLOCKS, LRU_BLOCK, LRU_BLOCK), LRU_BLOCK ** -0.5),
        'b_rg_a': nrm((L, D_RNN), 0.01),
        'w_rg_x': nrm((L, N_LRU_BLOCKS, LRU_BLOCK, LRU_BLOCK), LRU_BLOCK ** -0.5),
        'b_rg_x': nrm((L, D_RNN), 0.01),
        'lru_lambda': lru_lambda,
        'g_q': 1.0 + nrm((L, HEAD_DIM), 0.02),
        'g_k': 1.0 + nrm((L, HEAD_DIM), 0.02),
        'w_br_lru': nrm((L, D_RNN, D_MODEL), D_RNN ** -0.5),
        'w_br_att': nrm((L, ATT_OUT, D_MODEL), ATT_OUT ** -0.5),
        'w_out': nrm((L, D_MODEL, D_MODEL), D_MODEL ** -0.5),
        'g_ffn': 1.0 + nrm((L, D_MODEL), 0.02),
        'w_router': nrm((L, D_MODEL, N_EXPERTS), D_MODEL ** -0.5),
        'b_router': nrm((L, N_EXPERTS), 0.01),
        'w_gu': nrm((L, N_EXPERTS, D_MODEL, 2 * D_FF), D_MODEL ** -0.5),
        'b_gu': nrm((L, N_EXPERTS, 2 * D_FF), 0.01),
        'w_down': nrm((L, N_EXPERTS, D_FF, D_MODEL), D_FF ** -0.5),
        'b_down': nrm((L, N_EXPERTS, D_MODEL), 0.01),
        'g_ple': 1.0 + nrm((L, D_MODEL), 0.02),
        'w_ple_gate': nrm((L, D_MODEL, D_MODEL), D_MODEL ** -0.5),
        'w_ple': nrm((L, PLE_DIM, D_MODEL), PLE_DIM ** -0.5),
    }


def reference(x_prompt, x_sample, cache_kv_g0, cache_kv_g1, cache_kv_g2, state_conv, state_h, p_prompt, p_sample,
              g_mix, w_in, conv_w, conv_b, w_rg_a, b_rg_a, w_rg_x, b_rg_x, lru_lambda, g_q, g_k,
              w_br_lru, w_br_att, w_out, g_ffn, w_router, b_router, w_gu, b_gu, w_down, b_down,
              g_ple, w_ple_gate, w_ple):
    B, S, _ = x_prompt.shape
    T = x_sample.shape[1]
    pos_p = jnp.arange(S, dtype=jnp.int32)
    pos_s = PAST_LEN + jnp.arange(T, dtype=jnp.int32)
    xp, xs = x_prompt, x_sample
    kvp = [[], [], []]
    kvs = [[], [], []]
    conv_p, h_p, conv_s, h_s = [], [], [], []
    for l in range(DEPTH):
        lw = dict(g_mix=g_mix[l], w_in=w_in[l], conv_w=conv_w[l], conv_b=conv_b[l], w_rg_a=w_rg_a[l],
                  b_rg_a=b_rg_a[l], w_rg_x=w_rg_x[l], b_rg_x=b_rg_x[l], lru_lambda=lru_lambda[l],
                  g_q=g_q[l], g_k=g_k[l], w_br_lru=w_br_lru[l], w_br_att=w_br_att[l], w_out=w_out[l],
                  g_ffn=g_ffn[l], w_router=w_router[l], b_router=b_router[l], w_gu=w_gu[l], b_gu=b_gu[l],
                  w_down=w_down[l], b_down=b_down[l], g_ple=g_ple[l], w_ple_gate=w_ple_gate[l], w_ple=w_ple[l])
        conv0 = jnp.zeros((B, CONV_W - 1, D_RNN), xp.dtype)
        h0 = jnp.zeros((B, D_RNN), xp.dtype)
        xp, kv_new_p, cp, hp = layer(xp, p_prompt[l], conv0, h0, pos_p, attend_prompt, lw)
        caches = (cache_kv_g0[l], cache_kv_g1[l], cache_kv_g2[l])
        xs, kv_new_s, cs, hs = layer(xs, p_sample[l], state_conv[l], state_h[l], pos_s,
                                     functools.partial(attend_sample, caches=caches), lw)
        for g in range(N_GROUPS):
            kvp[g].append(kv_new_p[g])
            kvs[g].append(kv_new_s[g])
        conv_p.append(cp)
        h_p.append(hp)
        conv_s.append(cs)
        h_s.append(hs)
    return (xp, xs,
            jnp.stack(kvp[0]), jnp.stack(kvp[1]), jnp.stack(kvp[2]), jnp.stack(conv_p), jnp.stack(h_p),
            jnp.stack(kvs[0]), jnp.stack(kvs[1]), jnp.stack(kvs[2]), jnp.stack(conv_s), jnp.stack(h_s))
```

```python
import functools

import jax
import jax.numpy as jnp
from jax import lax
from jax.experimental import pallas as pl
from jax.experimental.pallas import tpu as pltpu

D_MODEL = 1024
D_RNN = 1024
N_LRU_BLOCKS = 4
LRU_BLOCK = D_RNN // N_LRU_BLOCKS
CONV_W = 4
LRU_C = 8.0
HEAD_DIM = 128
HEADS_PER_GROUP = 4
GROUPS = ((128, 1), (512, 4), (2048, 16))
N_GROUPS = len(GROUPS)
GROUP_W = HEADS_PER_GROUP * HEAD_DIM
QKV_W = N_GROUPS * GROUP_W
ROT_DIM = HEAD_DIM // 4
ROPE_THETA = 500000.0
N_EXPERTS = 32
TOP_K = 4
D_FF = 1024
SWIGLU_LIMIT = 7.0
SWIGLU_ALPHA = 1.702
PLE_DIM = 256
EPS = 1e-6
PAST_LEN = 8192

OFF_XR = 0
OFF_GR = OFF_XR + D_RNN
OFF_Q = OFF_GR + D_RNN
OFF_K = OFF_Q + QKV_W
OFF_V = OFF_K + QKV_W
OFF_GLRU = OFF_V + QKV_W
OFF_GATT = OFF_GLRU + D_MODEL
N_IN = OFF_GATT + D_MODEL

LANES = 128
SUBLANES = 8
ROW_TILES = D_MODEL // LANES
ATT_BLOCK = 128
TM = 512
VMEM_LIMIT = 56 * 1024 * 1024

F32 = jnp.float32
BF16 = jnp.bfloat16


def _cparams(sem):
    return pltpu.CompilerParams(dimension_semantics=sem, vmem_limit_bytes=VMEM_LIMIT)


def _resident(shape):
    nd = len(shape)
    return pl.BlockSpec(shape, lambda *_: (0,) * nd, pipeline_mode=pl.Buffered(1))


def _dot(a, b):
    return jnp.dot(a, b, preferred_element_type=F32)


def _rms(x, g):
    return x * lax.rsqrt(jnp.mean(x * x, axis=-1, keepdims=True) + EPS) * g


def _rope_tables(pos):
    half = ROT_DIM // 2
    freqs = jnp.float32(ROPE_THETA) ** (-jnp.arange(half, dtype=F32) * (2.0 / ROT_DIM))
    ang = pos.astype(F32)[:, None] * freqs[None, :]
    cos, sin = jnp.cos(ang), jnp.sin(ang)
    t = pos.shape[0]
    c = jnp.concatenate([cos, cos, jnp.ones((t, HEAD_DIM - ROT_DIM), F32)], axis=-1)
    s_lo = jnp.concatenate([jnp.zeros((t, half), F32), sin, jnp.zeros((t, HEAD_DIM - ROT_DIM), F32)], axis=-1)
    s_hi = jnp.concatenate([-sin, jnp.zeros((t, HEAD_DIM - half), F32)], axis=-1)
    return jnp.stack([c, s_lo, s_hi])


def _qk_prep(z, g, rope):
    half = ROT_DIM // 2
    c, s_lo, s_hi = rope[0], rope[1], rope[2]
    outs = []
    for h in range(HEADS_PER_GROUP):
        xh = z[:, h * HEAD_DIM:(h + 1) * HEAD_DIM]
        xn = _rms(xh, g)
        outs.append(xn * c + pltpu.roll(xn, half, 1) * s_lo + pltpu.roll(xn, HEAD_DIM - half, 1) * s_hi)
    return jnp.concatenate(outs, axis=-1)


def _lru_gates(xc, wa_ref, ba, wx_ref, bx, lam):
    xcb = xc.astype(BF16)
    rs, xs = [], []
    for n in range(N_LRU_BLOCKS):
        xb = xcb[:, n * LRU_BLOCK:(n + 1) * LRU_BLOCK]
        rs.append(_dot(xb, wa_ref[n]))
        xs.append(_dot(xb, wx_ref[n]))
    r = jax.nn.sigmoid(jnp.concatenate(rs, axis=-1) + ba)
    i = jax.nn.sigmoid(jnp.concatenate(xs, axis=-1) + bx)
    log_a = -LRU_C * r * jax.nn.softplus(-lam)
    a = jnp.exp(log_a)
    u = jnp.sqrt(-jnp.tanh(log_a) * (a * a + 1.0)) * (i * xc)
    return a, u


def _branch_a_kernel(x_ref, gmix_ref, w_ref, cw_ref, cb_ref, wa_ref, ba_ref, wx_ref, bx_ref, lam_ref, wbr_ref,
                     ma_ref, convn_ref, hn_ref, xe_ref, a_ref, u_ref, hc_ref, *, tm, ns):
    s = pl.program_id(1)

    @pl.when(s == 0)
    def _():
        xe_ref[0:SUBLANES, :] = jnp.zeros((SUBLANES, D_RNN), F32)
        hc_ref[...] = jnp.zeros((1, D_RNN), F32)

    h = _rms(x_ref[...], gmix_ref[...]).astype(BF16)
    xr = _dot(h, w_ref[:, 0:D_RNN])
    xe_ref[SUBLANES:SUBLANES + tm, :] = xr
    cw = cw_ref[...]
    xc = (cw[0:1] * xe_ref[5:5 + tm, :] + cw[1:2] * xe_ref[6:6 + tm, :] + cw[2:3] * xe_ref[7:7 + tm, :]
          + cw[3:4] * xr + cb_ref[...])
    xe_ref[0:SUBLANES, :] = xe_ref[tm:tm + SUBLANES, :]

    a, u = _lru_gates(xc, wa_ref, ba_ref[...], wx_ref, bx_ref[...], lam_ref[...])
    a_ref[...] = a
    u_ref[...] = u
    row = lax.broadcasted_iota(jnp.int32, (SUBLANES, D_RNN), 0)

    def scan_body(g, carry):
        r0 = pl.multiple_of(g * SUBLANES, SUBLANES)
        av = a_ref[pl.ds(r0, SUBLANES), :]
        uv = u_ref[pl.ds(r0, SUBLANES), :]
        for sh in (1, 2, 4):
            a_sh = jnp.where(row >= sh, pltpu.roll(av, sh, 0), 1.0)
            u_sh = jnp.where(row >= sh, pltpu.roll(uv, sh, 0), 0.0)
            uv = uv + av * u_sh
            av = av * a_sh
        hv = uv + av * carry
        u_ref[pl.ds(r0, SUBLANES), :] = hv
        return hv[SUBLANES - 1:SUBLANES, :]

    h_last = lax.fori_loop(0, tm // SUBLANES, scan_body, hc_ref[...])
    hc_ref[...] = h_last

    @pl.when(s == ns - 1)
    def _():
        hn_ref[0] = h_last
        convn_ref[0] = xr[tm - (CONV_W - 1):tm, :]

    gr = _dot(h, w_ref[:, D_RNN:2 * D_RNN])
    y = (u_ref[...] * jax.nn.gelu(gr)).astype(BF16)
    b_lru = _dot(y, wbr_ref[...])
    g_lru = _dot(h, w_ref[:, 2 * D_RNN:2 * D_RNN + D_MODEL])
    ma_ref[...] = (jax.nn.sigmoid(g_lru) * b_lru).astype(BF16)


def _branch_a(x2d, bsz, seq, gmix, w_a, cw, cb, wa, ba, wx, bx, lam, wbr):
    tm = TM
    ns = seq // tm
    kern = functools.partial(_branch_a_kernel, tm=tm, ns=ns)
    return pl.pallas_call(
        kern,
        grid=(bsz, ns),
        in_specs=[
            pl.BlockSpec((tm, D_MODEL), lambda b, s: (b * ns + s, 0)),
            _resident((1, D_MODEL)),
            _resident(w_a.shape),
            _resident((CONV_W, D_RNN)),
            _resident((1, D_RNN)),
            _resident(wa.shape),
            _resident((1, D_RNN)),
            _resident(wx.shape),
            _resident((1, D_RNN)),
            _resident((1, D_RNN)),
            _resident(wbr.shape),
        ],
        out_specs=[
            pl.BlockSpec((tm, D_MODEL), lambda b, s: (b * ns + s, 0)),
            pl.BlockSpec((1, CONV_W - 1, D_RNN), lambda b, s: (b, 0, 0)),
            pl.BlockSpec((1, 1, D_RNN), lambda b, s: (b, 0, 0)),
        ],
        out_shape=[
            jax.ShapeDtypeStruct((bsz * seq, D_MODEL), BF16),
            jax.ShapeDtypeStruct((bsz, CONV_W - 1, D_RNN), F32),
            jax.ShapeDtypeStruct((bsz, 1, D_RNN), F32),
        ],
        scratch_shapes=[
            pltpu.VMEM((tm + SUBLANES, D_RNN), F32),
            pltpu.VMEM((tm, D_RNN), F32),
            pltpu.VMEM((tm, D_RNN), F32),
            pltpu.VMEM((1, D_RNN), F32),
        ],
        compiler_params=_cparams(("arbitrary", "arbitrary")),
        name="branch_a_prompt",
    )(x2d, gmix, w_a, cw, cb, wa, ba, wx, bx, lam, wbr)


def _qkv_kernel(x_ref, gmix_ref, w_ref, gq_ref, gk_ref, rope_ref, *out_refs, tm, ns):
    q_refs = out_refs[0:3]
    k_refs = out_refs[3:6]
    v_refs = out_refs[6:9]
    sg_ref, kv2_ref, kv1_ref, kv0_ref = out_refs[9:13]
    s = pl.program_id(1)
    h = _rms(x_ref[...], gmix_ref[...]).astype(BF16)
    rope = rope_ref[...]
    gq = gq_ref[...]
    gk = gk_ref[...]
    for g in range(N_GROUPS):
        qg = _qk_prep(_dot(h, w_ref[:, g * GROUP_W:(g + 1) * GROUP_W]), gq, rope)
        kg = _qk_prep(_dot(h, w_ref[:, QKV_W + g * GROUP_W:QKV_W + (g + 1) * GROUP_W]), gk, rope)
        vg = _dot(h, w_ref[:, 2 * QKV_W + g * GROUP_W:2 * QKV_W + (g + 1) * GROUP_W])
        q_refs[g][...] = qg.astype(BF16)
        k_refs[g][...] = kg.astype(BF16)
        v_refs[g][...] = vg.astype(BF16)
        if g == 2:
            kv2_ref[0, :, 0:GROUP_W] = kg
            kv2_ref[0, :, GROUP_W:2 * GROUP_W] = vg
        elif g == 1:
            @pl.when(s == ns - 1)
            def _(kg=kg, vg=vg):
                kv1_ref[0, :, 0:GROUP_W] = kg
                kv1_ref[0, :, GROUP_W:2 * GROUP_W] = vg
        else:
            @pl.when(s == ns - 1)
            def _(kg=kg, vg=vg):
                w0 = GROUPS[0][0]
                kv0_ref[0, :, 0:GROUP_W] = kg[tm - w0:tm, :]
                kv0_ref[0, :, GROUP_W:2 * GROUP_W] = vg[tm - w0:tm, :]
    g_att = _dot(h, w_ref[:, 3 * QKV_W:3 * QKV_W + D_MODEL])
    sg_ref[...] = jax.nn.sigmoid(g_att).astype(BF16)


def _qkv(x2d, bsz, seq, gmix, w_b, gq, gk, rope):
    tm = TM
    ns = seq // tm
    assert GROUPS[1][0] == tm and GROUPS[2][0] == seq and GROUPS[0][0] <= tm
    n = bsz * seq
    kern = functools.partial(_qkv_kernel, tm=tm, ns=ns)
    row_spec = pl.BlockSpec((tm, GROUP_W), lambda b, s: (b * ns + s, 0))
    return pl.pallas_call(
        kern,
        grid=(bsz, ns),
        in_specs=[
            pl.BlockSpec((tm, D_MODEL), lambda b, s: (b * ns + s, 0)),
            _resident((1, D_MODEL)),
            _resident(w_b.shape),
            _resident((1, HEAD_DIM)),
            _resident((1, HEAD_DIM)),
            pl.BlockSpec((3, tm, HEAD_DIM), lambda b, s: (0, s, 0)),
        ],
        out_specs=[row_spec] * 9 + [
            pl.BlockSpec((tm, D_MODEL), lambda b, s: (b * ns + s, 0)),
            pl.BlockSpec((1, tm, 2 * GROUP_W), lambda b, s: (b, s, 0)),
            pl.BlockSpec((1, tm, 2 * GROUP_W), lambda b, s: (b, 0, 0)),
            pl.BlockSpec((1, GROUPS[0][0], 2 * GROUP_W), lambda b, s: (b, 0, 0)),
        ],
        out_shape=[jax.ShapeDtypeStruct((n, GROUP_W), BF16)] * 9 + [
            jax.ShapeDtypeStruct((n, D_MODEL), BF16),
            jax.ShapeDtypeStruct((bsz, seq, 2 * GROUP_W), F32),
            jax.ShapeDtypeStruct((bsz, GROUPS[1][0], 2 * GROUP_W), F32),
            jax.ShapeDtypeStruct((bsz, GROUPS[0][0], 2 * GROUP_W), F32),
        ],
        compiler_params=_cparams(("arbitrary", "arbitrary")),
        name="qkv_prompt",
    )(x2d, gmix, w_b, gq, gk, rope)


def _dot_nt(a, b):
    return lax.dot_general(a, b, (((1,), (1,)), ((), ())), preferred_element_type=F32)


def _attn_kernel(q_ref, k_ref, v_ref, o_ref, lse_ref, *, dil, sub_len):
    nb = sub_len // ATT_BLOCK
    scale = HEAD_DIM ** -0.5
    ri = lax.broadcasted_iota(jnp.int32, (ATT_BLOCK, ATT_BLOCK), 0)
    ci = lax.broadcasted_iota(jnp.int32, (ATT_BLOCK, ATT_BLOCK), 1)
    cur_mask = ci <= ri
    prev_mask = ci >= ri
    for r in range(dil):
        for hh in range(HEADS_PER_GROUP):
            c0 = r * GROUP_W + hh * HEAD_DIM
            col = r * HEADS_PER_GROUP + hh

            def blk(n, carry, c0=c0, col=col):
                m0 = pl.multiple_of(n * ATT_BLOCK, ATT_BLOCK)
                q = q_ref[0, pl.ds(m0, ATT_BLOCK), c0:c0 + HEAD_DIM]
                kc = k_ref[0, pl.ds(m0, ATT_BLOCK), c0:c0 + HEAD_DIM]
                vc = v_ref[0, pl.ds(m0, ATT_BLOCK), c0:c0 + HEAD_DIM]
                s_c = jnp.where(cur_mask, _dot_nt(q, kc) * scale, -jnp.inf)
                m = jnp.max(s_c, axis=-1, keepdims=True)
                if nb > 1:
                    mp = pl.multiple_of(jnp.maximum(n - 1, 0) * ATT_BLOCK, ATT_BLOCK)
                    kp = k_ref[0, pl.ds(mp, ATT_BLOCK), c0:c0 + HEAD_DIM]
                    vp = v_ref[0, pl.ds(mp, ATT_BLOCK), c0:c0 + HEAD_DIM]
                    no_prev = jnp.where(n > 0, 0.0, -jnp.inf)
                    s_p = jnp.where(prev_mask, _dot_nt(q, kp) * scale, -jnp.inf) + no_prev
                    m = jnp.maximum(m, jnp.max(s_p, axis=-1, keepdims=True))
                    p_p = jnp.exp(s_p - m)
                p_c = jnp.exp(s_c - m)
                l = jnp.sum(p_c, axis=-1, keepdims=True)
                o = _dot(p_c.astype(BF16), vc)
                if nb > 1:
                    l = l + jnp.sum(p_p, axis=-1, keepdims=True)
                    o = o + _dot(p_p.astype(BF16), vp)
                o_ref[0, pl.ds(m0, ATT_BLOCK), c0:c0 + HEAD_DIM] = (o / l).astype(BF16)
                lse_ref[0, pl.ds(m0, ATT_BLOCK), col:col + 1] = m + jnp.log(l)
                return carry

            lax.fori_loop(0, nb, blk, 0)


def _attention_group(q, k, v, bsz, seq, dil):
    sub_len = seq // dil
    width = dil * GROUP_W
    view = lambda t: t.reshape(bsz, sub_len, width)
    kern = functools.partial(_attn_kernel, dil=dil, sub_len=sub_len)
    spec = pl.BlockSpec((1, sub_len, width), lambda b: (b, 0, 0))
    o, lse = pl.pallas_call(
        kern,
        grid=(bsz,),
        in_specs=[spec, spec, spec],
        out_specs=[spec, pl.BlockSpec((1, sub_len, dil * HEADS_PER_GROUP), lambda b: (b, 0, 0))],
        out_shape=[jax.ShapeDtypeStruct((bsz, sub_len, width), BF16),
                   jax.ShapeDtypeStruct((bsz, sub_len, dil * HEADS_PER_GROUP), F32)],
        compiler_params=_cparams(("arbitrary",)),
        name=f"attention_dil{dil}",
    )(view(q), view(k), view(v))
    return o.reshape(bsz * seq, GROUP_W), lse.reshape(bsz * seq, HEADS_PER_GROUP)


def _merge_kernel(*refs, tm, combine):
    if combine:
        (x_ref, ma_ref, sg_ref, o0_ref, o1_ref, o2_ref, l0_ref, l1_ref, l2_ref,
         wbatt_ref, wout_ref, gffn_ref, wr_ref, br_ref,
         x1_ref, hf_ref, e_ref, gate_ref, rank_ref, cnt_ref) = refs
        l0, l1, l2 = l0_ref[...], l1_ref[...], l2_ref[...]
        mx = jnp.maximum(jnp.maximum(l0, l1), l2)
        e0, e1, e2 = jnp.exp(l0 - mx), jnp.exp(l1 - mx), jnp.exp(l2 - mx)
        den = e0 + e1 + e2
        ws = (e0 / den, e1 / den, e2 / den)
        os_ = (o0_ref, o1_ref, o2_ref)
        parts = []
        for hh in range(HEADS_PER_GROUP):
            acc = None
            for g in range(N_GROUPS):
                t = ws[g][:, hh:hh + 1] * os_[g][:, hh * HEAD_DIM:(hh + 1) * HEAD_DIM].astype(F32)
                acc = t if acc is None else acc + t
            parts.append(acc)
        att = jnp.concatenate(parts, axis=-1)
    else:
        (x_ref, ma_ref, sg_ref, att_ref,
         wbatt_ref, wout_ref, gffn_ref, wr_ref, br_ref,
         x1_ref, hf_ref, e_ref, gate_ref, rank_ref, cnt_ref) = refs
        att = att_ref[...]
    i = pl.program_id(0)

    @pl.when(i == 0)
    def _():
        cnt_ref[...] = jnp.zeros((1, N_EXPERTS), F32)

    b_att = _dot(att.astype(BF16), wbatt_ref[...])
    merged = ma_ref[...].astype(F32) + sg_ref[...].astype(F32) * b_att
    x1 = x_ref[...] + _dot(merged.astype(BF16), wout_ref[...])
    x1_ref[...] = x1
    hf = _rms(x1, gffn_ref[...])
    for s in range(ROW_TILES):
        hf_ref[pl.ds(s, tm, stride=ROW_TILES), :] = hf[:, s * LANES:(s + 1) * LANES]

    wr = wr_ref[...]
    wr_hi = wr.astype(BF16)
    wr_lo = (wr - wr_hi.astype(F32)).astype(BF16)
    hf_hi = hf.astype(BF16)
    hf_lo = (hf - hf_hi.astype(F32)).astype(BF16)
    logits = _dot(hf_hi, wr_hi) + (_dot(hf_hi, wr_lo) + _dot(hf_lo, wr_hi)) + br_ref[...]

    lane = lax.broadcasted_iota(jnp.int32, (tm, N_EXPERTS), 1)
    work = logits
    es, vs = [], []
    for _k in range(TOP_K):
        m = jnp.max(work, axis=-1, keepdims=True)
        idx = jnp.min(jnp.where(work == m, lane, N_EXPERTS), axis=-1, keepdims=True)
        es.append(idx)
        vs.append(m)
        work = jnp.where(lane == idx, -jnp.inf, work)
    top_v = jnp.concatenate(vs, axis=-1)
    ex = jnp.exp(top_v - top_v[:, 0:1])
    gate_ref[...] = ex / jnp.sum(ex, axis=-1, keepdims=True)
    e_ref[...] = jnp.concatenate(es, axis=-1)

    chosen = jnp.zeros((tm, N_EXPERTS), F32)
    for k in range(TOP_K):
        chosen = chosen + jnp.where(lane == es[k], 1.0, 0.0)
    ri = lax.broadcasted_iota(jnp.int32, (tm, tm), 0)
    ci = lax.broadcasted_iota(jnp.int32, (tm, tm), 1)
    tri = jnp.where(ci < ri, 1.0, 0.0).astype(BF16)
    before = _dot(tri, chosen.astype(BF16)) + cnt_ref[...]
    ranks = [jnp.sum(jnp.where(lane == es[k], before, 0.0), axis=-1, keepdims=True) for k in range(TOP_K)]
    rank_ref[...] = jnp.concatenate(ranks, axis=-1).astype(jnp.int32)
    cnt_ref[...] = cnt_ref[...] + jnp.sum(chosen, axis=0, keepdims=True)


def _merge_route(x2d, ma, sg, att_inputs, wbatt, wout, gffn, wr, br, tm, combine, name):
    n = x2d.shape[0]
    kern = functools.partial(_merge_kernel, tm=tm, combine=combine)
    rows = lambda w: pl.BlockSpec((tm, w), lambda i: (i, 0))
    if combine:
        att_specs = [rows(GROUP_W)] * 3 + [rows(HEADS_PER_GROUP)] * 3
    else:
        att_specs = [rows(GROUP_W)]
    return pl.pallas_call(
        kern,
        grid=(n // tm,),
        in_specs=[rows(D_MODEL), rows(D_MODEL), rows(D_MODEL)] + att_specs + [
            _resident(wbatt.shape), _resident(wout.shape), _resident((1, D_MODEL)),
            _resident(wr.shape), _resident((1, N_EXPERTS))],
        out_specs=[
            rows(D_MODEL),
            pl.BlockSpec((tm * ROW_TILES, LANES), lambda i: (i, 0)),
            rows(TOP_K), rows(TOP_K), rows(TOP_K),
            pl.BlockSpec((1, N_EXPERTS), lambda i: (0, 0)),
        ],
        out_shape=[
            jax.ShapeDtypeStruct((n, D_MODEL), F32),
            jax.ShapeDtypeStruct((n * ROW_TILES, LANES), F32),
            jax.ShapeDtypeStruct((n, TOP_K), jnp.int32),
            jax.ShapeDtypeStruct((n, TOP_K), F32),
            jax.ShapeDtypeStruct((n, TOP_K), jnp.int32),
            jax.ShapeDtypeStruct((1, N_EXPERTS), F32),
        ],
        compiler_params=_cparams(("arbitrary",)),
        name=name,
    )(x2d, ma, sg, *att_inputs, wbatt, wout, gffn, wr, br)


def _expert_kernel(blk_e_ref, nvalid_ref, nused_ref, src_ref, srcn_ref, dst_ref, gate_ref, hf_any, wgu_ref,
                   bgu_ref, wd_ref, bd_ref, out_any, xbuf, ybuf, gsem, ssem, *, bm, nblk):
    del blk_e_ref
    i = pl.program_id(0)
    nu = nused_ref[0]
    slot = i % 2
    unroll = 8

    def gather_row(idx_ref, sl, r):
        return pltpu.make_async_copy(hf_any.at[idx_ref[0, 0, r]], xbuf.at[sl, r], gsem.at[sl])

    def issue_gather(idx_ref, sl):
        def body(j, c):
            for uu in range(unroll):
                gather_row(idx_ref, sl, j * unroll + uu).start()
            return c
        lax.fori_loop(0, bm // unroll, body, 0)

    def wait_gather(sl):
        pltpu.make_async_copy(xbuf.at[sl], xbuf.at[sl], gsem.at[sl]).wait()

    def scatter_row(sl, r):
        return pltpu.make_async_copy(ybuf.at[sl, r], out_any.at[dst_ref[0, 0, r]], ssem.at[sl])

    def wait_scatter(blk):
        sl = blk % 2
        cnt = nvalid_ref[blk]

        @pl.when(cnt > 0)
        def _():
            pltpu.make_async_copy(ybuf.at[sl, pl.ds(0, cnt)], ybuf.at[sl, pl.ds(0, cnt)], ssem.at[sl]).wait()

    @pl.when(jnp.logical_and(i == 0, nu > 0))
    def _():
        issue_gather(src_ref, 0)

    @pl.when(i + 1 < nu)
    def _():
        issue_gather(srcn_ref, 1 - slot)

    @pl.when(i < nu)
    def _():
        wait_gather(slot)
        x = jnp.concatenate([xbuf[slot, :, s, :] for s in range(ROW_TILES)], axis=-1).astype(BF16)
        gu = _dot(x, wgu_ref[0]) + bgu_ref[0]
        gate = jnp.minimum(gu[:, :D_FF], SWIGLU_LIMIT)
        up = jnp.clip(gu[:, D_FF:], -SWIGLU_LIMIT, SWIGLU_LIMIT)
        hmid = (up + 1.0) * (gate * jax.nn.sigmoid(SWIGLU_ALPHA * gate))
        y = (_dot(hmid.astype(BF16), wd_ref[0]) + bd_ref[0]) * gate_ref[...]

        @pl.when(i >= 2)
        def _():
            wait_scatter(i - 2)

        for s in range(ROW_TILES):
            ybuf[slot, :, s, :] = y[:, s * LANES:(s + 1) * LANES]

        cnt = nvalid_ref[i]
        full = cnt // unroll

        def body(j, c):
            for uu in range(unroll):
                scatter_row(slot, j * unroll + uu).start()
            return c
        lax.fori_loop(0, full, body, 0)

        def tail(r, c):
            scatter_row(slot, r).start()
            return c
        lax.fori_loop(full * unroll, cnt, tail, 0)

    @pl.when(i == nblk - 1)
    def _():
        @pl.when(nu >= 2)
        def _():
            wait_scatter(nu - 2)

        @pl.when(nu >= 1)
        def _():
            wait_scatter(nu - 1)


def _experts(hf_rows, blk_e, nvalid, nused, row_src, row_dst, row_gate, wgu, bgu, wd, bd, n_slots, bm):
    r_total = row_src.shape[0]
    nblk = r_total // bm
    kern = functools.partial(_expert_kernel, bm=bm, nblk=nblk)
    idx3 = lambda a: a.reshape(nblk, 1, bm)
    smem_blk = lambda f: pl.BlockSpec((1, 1, bm), f, memory_space=pltpu.SMEM)
    grid_spec = pltpu.PrefetchScalarGridSpec(
        num_scalar_prefetch=3,
        grid=(nblk,),
        in_specs=[
            smem_blk(lambda i, be, nv, nu: (i, 0, 0)),
            smem_blk(lambda i, be, nv, nu: (jnp.minimum(i + 1, nblk - 1), 0, 0)),
            smem_blk(lambda i, be, nv, nu: (i, 0, 0)),
            pl.BlockSpec((bm, 1), lambda i, be, nv, nu: (i, 0)),
            pl.BlockSpec(memory_space=pl.ANY),
            pl.BlockSpec((1, D_MODEL, 2 * D_FF), lambda i, be, nv, nu: (be[i], 0, 0)),
            pl.BlockSpec((1, 1, 2 * D_FF), lambda i, be, nv, nu: (be[i], 0, 0)),
            pl.BlockSpec((1, D_FF, D_MODEL), lambda i, be, nv, nu: (be[i], 0, 0)),
            pl.BlockSpec((1, 1, D_MODEL), lambda i, be, nv, nu: (be[i], 0, 0)),
        ],
        out_specs=pl.BlockSpec(memory_space=pl.ANY),
        scratch_shapes=[
            pltpu.VMEM((2, bm, ROW_TILES, LANES), F32),
            pltpu.VMEM((2, bm, ROW_TILES, LANES), F32),
            pltpu.SemaphoreType.DMA((2,)),
            pltpu.SemaphoreType.DMA((2,)),
        ],
    )
    return pl.pallas_call(
        kern,
        grid_spec=grid_spec,
        out_shape=jax.ShapeDtypeStruct((n_slots, ROW_TILES, LANES), F32),
        compiler_params=_cparams(("arbitrary",)),
        name=f"experts_bm{bm}",
    )(blk_e, nvalid, nused, idx3(row_src), idx3(row_src), idx3(row_dst), row_gate.reshape(r_total, 1), hf_rows,
      wgu, bgu, wd, bd)


def _route_plan(top_e, rank, counts, gates, bm):
    n = top_e.shape[0]
    na = n * TOP_K
    r_total = na + N_EXPERTS * bm
    nblk = r_total // bm
    cnt = counts.reshape(N_EXPERTS).astype(jnp.int32)
    padded = (cnt + bm - 1) // bm * bm
    pad_end = jnp.cumsum(padded)
    pad_start = pad_end - padded
    dest = (pad_start[top_e] + rank).reshape(na)
    inv = jnp.full((r_total,), -1, jnp.int32).at[dest].set(jnp.arange(na, dtype=jnp.int32))
    valid = inv >= 0
    tok = inv // TOP_K
    slot_k = inv % TOP_K
    row_src = jnp.where(valid, tok, 0)
    row_dst = jnp.where(valid, slot_k * n + tok, 0)
    row_gate = jnp.where(valid, gates.reshape(na)[jnp.maximum(inv, 0)], 0.0)
    blk_start = jnp.arange(nblk, dtype=jnp.int32) * bm
    blk_e = jnp.minimum(jnp.searchsorted(pad_end, blk_start, side='right'), N_EXPERTS - 1).astype(jnp.int32)
    nused = (pad_end[-1] // bm).astype(jnp.int32)
    nvalid = jnp.clip(cnt[blk_e] - (blk_start - pad_start[blk_e]), 0, bm)
    nvalid = jnp.where(jnp.arange(nblk, dtype=jnp.int32) < nused, nvalid, 0).astype(jnp.int32)
    return blk_e, nvalid, nused.reshape(1), row_src, row_dst, row_gate, na


def _final_kernel(x1_ref, s0_ref, s1_ref, s2_ref, s3_ref, p_ref, gple_ref, wpg_ref, wple_ref, y_ref, *, tm):
    moe = None
    for s_ref in (s0_ref, s1_ref, s2_ref, s3_ref):
        t = jnp.concatenate([s_ref[pl.ds(s, tm, stride=ROW_TILES), :] for s in range(ROW_TILES)], axis=-1)
        moe = t if moe is None else moe + t
    x2 = x1_ref[...] + moe
    hg = _rms(x2, gple_ref[...]).astype(BF16)
    gate = jax.nn.sigmoid(_dot(hg, wpg_ref[...]))
    pe = _dot(p_ref[...].astype(BF16), wple_ref[...])
    y_ref[...] = x2 + gate * pe


def _final(x1, slots, p2d, gple, wpg, wple, tm, name):
    n = x1.shape[0]
    nt = n // tm
    kern = functools.partial(_final_kernel, tm=tm)
    slots2d = slots.reshape(slots.shape[0] * ROW_TILES, LANES)
    slot_spec = lambda k: pl.BlockSpec((tm * ROW_TILES, LANES), lambda i: (k * nt + i, 0))
    return pl.pallas_call(
        kern,
        grid=(nt,),
        in_specs=[pl.BlockSpec((tm, D_MODEL), lambda i: (i, 0))] + [slot_spec(k) for k in range(TOP_K)] + [
            pl.BlockSpec((tm, PLE_DIM), lambda i: (i, 0)),
            _resident((1, D_MODEL)), _resident(wpg.shape), _resident(wple.shape)],
        out_specs=pl.BlockSpec((tm, D_MODEL), lambda i: (i, 0)),
        out_shape=jax.ShapeDtypeStruct((n, D_MODEL), F32),
        compiler_params=_cparams(("arbitrary",)),
        name=name,
    )(x1, slots2d, slots2d, slots2d, slots2d, p2d, gple, wpg, wple)


def _moe_and_ple(x1, hf_rows, top_e, gates, rank, counts, p2d, wts, bm, tm, tag):
    n = x1.shape[0]
    blk_e, nvalid, nused, row_src, row_dst, row_gate, n_slots = _route_plan(top_e, rank, counts, gates, bm)
    slots = _experts(hf_rows.reshape(n, ROW_TILES, LANES), blk_e, nvalid, nused, row_src, row_dst, row_gate,
                     wts['w_gu'], wts['b_gu'], wts['w_down'], wts['b_down'], n_slots, bm)
    return _final(x1, slots, p2d, wts['g_ple'], wts['w_ple_gate'], wts['w_ple'], tm, f"final_{tag}")


def _sample_in_kernel(x_ref, gmix_ref, wa_in_ref, wb_in_ref, cw_ref, cb_ref, wa_ref, ba_ref, wx_ref, bx_ref,
                      lam_ref, wbr_ref, gq_ref, gk_ref, rope_ref, conv_ref, h0_ref,
                      ma_ref, sg_ref, q_ref, kv0_ref, kv1_ref, kv2_ref, convn_ref, hn_ref):
    h = _rms(x_ref[...], gmix_ref[...]).astype(BF16)
    xr = _dot(h, wa_in_ref[:, 0:D_RNN])
    cw = cw_ref[...]
    conv = conv_ref[...]
    b0, b1, b2 = conv[:, 0:D_RNN], conv[:, D_RNN:2 * D_RNN], conv[:, 2 * D_RNN:3 * D_RNN]
    xc = cw[0:1] * b0 + cw[1:2] * b1 + cw[2:3] * b2 + cw[3:4] * xr + cb_ref[...]
    convn_ref[:, 0:D_RNN] = b1
    convn_ref[:, D_RNN:2 * D_RNN] = b2
    convn_ref[:, 2 * D_RNN:3 * D_RNN] = xr
    a, u = _lru_gates(xc, wa_ref, ba_ref[...], wx_ref, bx_ref[...], lam_ref[...])
    hn = a * h0_ref[...] + u
    hn_ref[...] = hn
    gr = _dot(h, wa_in_ref[:, D_RNN:2 * D_RNN])
    y = (hn * jax.nn.gelu(gr)).astype(BF16)
    b_lru = _dot(y, wbr_ref[...])
    g_lru = _dot(h, wa_in_ref[:, 2 * D_RNN:2 * D_RNN + D_MODEL])
    ma_ref[...] = (jax.nn.sigmoid(g_lru) * b_lru).astype(BF16)
    rope = rope_ref[...]
    kv_refs = (kv0_ref, kv1_ref, kv2_ref)
    for g in range(N_GROUPS):
        q_ref[:, g * GROUP_W:(g + 1) * GROUP_W] = _qk_prep(
            _dot(h, wb_in_ref[:, g * GROUP_W:(g + 1) * GROUP_W]), gq_ref[...], rope)
        kv_refs[g][:, 0:GROUP_W] = _qk_prep(
            _dot(h, wb_in_ref[:, QKV_W + g * GROUP_W:QKV_W + (g + 1) * GROUP_W]), gk_ref[...], rope)
        kv_refs[g][:, GROUP_W:2 * GROUP_W] = _dot(
            h, wb_in_ref[:, 2 * QKV_W + g * GROUP_W:2 * QKV_W + (g + 1) * GROUP_W])
    g_att = _dot(h, wb_in_ref[:, 3 * QKV_W:3 * QKV_W + D_MODEL])
    sg_ref[...] = jax.nn.sigmoid(g_att).astype(BF16)


def _sample_in(x2d, conv2d, h0, rope, wts):
    bs = x2d.shape[0]
    args = (x2d, wts['g_mix'], wts['w_in_a'], wts['w_in_b'], wts['conv_w'], wts['conv_b'], wts['w_rg_a'],
            wts['b_rg_a'], wts['w_rg_x'], wts['b_rg_x'], wts['lru_lambda'], wts['w_br_lru'], wts['g_q'],
            wts['g_k'], rope, conv2d, h0)
    whole = lambda a: pl.BlockSpec(a.shape, lambda i, nd=a.ndim: (0,) * nd)
    out_shape = [
        jax.ShapeDtypeStruct((bs, D_MODEL), BF16),
        jax.ShapeDtypeStruct((bs, D_MODEL), BF16),
        jax.ShapeDtypeStruct((bs, QKV_W), F32),
        jax.ShapeDtypeStruct((bs, 2 * GROUP_W), F32),
        jax.ShapeDtypeStruct((bs, 2 * GROUP_W), F32),
        jax.ShapeDtypeStruct((bs, 2 * GROUP_W), F32),
        jax.ShapeDtypeStruct((bs, (CONV_W - 1) * D_RNN), F32),
        jax.ShapeDtypeStruct((bs, D_RNN), F32),
    ]
    return pl.pallas_call(
        _sample_in_kernel,
        grid=(1,),
        in_specs=[whole(a) for a in args],
        out_specs=[whole(s) for s in out_shape],
        out_shape=out_shape,
        compiler_params=_cparams(("arbitrary",)),
        name="sample_in",
    )(*args)


def _sample_attn_kernel(q_ref, kn0_ref, kn1_ref, kn2_ref, c0_ref, c1_ref, c2_ref, att_ref):
    scale = HEAD_DIM ** -0.5
    q = q_ref[0]
    outs, lses = [], []
    for g, (kn_ref, c_ref) in enumerate(((kn0_ref, c0_ref), (kn1_ref, c1_ref), (kn2_ref, c2_ref))):
        kvn = kn_ref[0]
        kvc = c_ref[0]
        og, lg = [], []
        for hh in range(HEADS_PER_GROUP):
            qh = q[:, g * GROUP_W + hh * HEAD_DIM:g * GROUP_W + (hh + 1) * HEAD_DIM]
            kc = kvc[:, hh * HEAD_DIM:(hh + 1) * HEAD_DIM]
            vc = kvc[:, GROUP_W + hh * HEAD_DIM:GROUP_W + (hh + 1) * HEAD_DIM]
            kn = kvn[:, hh * HEAD_DIM:(hh + 1) * HEAD_DIM]
            vn = kvn[:, GROUP_W + hh * HEAD_DIM:GROUP_W + (hh + 1) * HEAD_DIM]
            s_c = jnp.sum(kc * qh, axis=-1, keepdims=True) * scale
            s_n = jnp.sum(kn * qh, axis=-1, keepdims=True) * scale
            m = jnp.maximum(jnp.max(s_c, axis=0, keepdims=True), s_n)
            p_c = jnp.exp(s_c - m)
            p_n = jnp.exp(s_n - m)
            l = jnp.sum(p_c, axis=0, keepdims=True) + p_n
            og.append((jnp.sum(p_c * vc, axis=0, keepdims=True) + p_n * vn) / l)
            lg.append(m + jnp.log(l))
        outs.append(og)
        lses.append(lg)
    parts = []
    for hh in range(HEADS_PER_GROUP):
        l0, l1, l2 = lses[0][hh], lses[1][hh], lses[2][hh]
        mx = jnp.maximum(jnp.maximum(l0, l1), l2)
        e0, e1, e2 = jnp.exp(l0 - mx), jnp.exp(l1 - mx), jnp.exp(l2 - mx)
        den = e0 + e1 + e2
        parts.append((e0 / den) * outs[0][hh] + (e1 / den) * outs[1][hh] + (e2 / den) * outs[2][hh])
    att_ref[0] = jnp.concatenate(parts, axis=-1)


def _sample_attention(q, kvn, caches):
    bs = q.shape[0]
    row3 = lambda a: a.reshape(bs, 1, a.shape[-1])
    row_spec = lambda w: pl.BlockSpec((1, 1, w), lambda b: (b, 0, 0))
    cache_views, cache_specs = [], []
    for g, (window, dil) in enumerate(GROUPS):
        nkeys = window // dil
        assert caches[g].shape[1] == window
        cache_views.append(caches[g].reshape(bs, nkeys, dil * 2 * GROUP_W))
        cache_specs.append(pl.BlockSpec((1, nkeys, 2 * GROUP_W), lambda b: (b, 0, 0)))
    att = pl.pallas_call(
        _sample_attn_kernel,
        grid=(bs,),
        in_specs=[row_spec(QKV_W)] + [row_spec(2 * GROUP_W)] * 3 + cache_specs,
        out_specs=row_spec(GROUP_W),
        out_shape=jax.ShapeDtypeStruct((bs, 1, GROUP_W), F32),
        compiler_params=_cparams(("arbitrary",)),
        name="sample_attention",
    )(row3(q), row3(kvn[0]), row3(kvn[1]), row3(kvn[2]), *cache_views)
    return att.reshape(bs, GROUP_W)


def _cache_shift_kernel(c0, c1, c2, n0, n1, n2, o0, o1, o2, sem, *, chunks):
    copies = []
    for c, nrow, o in ((c0, n0, o0), (c1, n1, o1), (c2, n2, o2)):
        bs, w = c.shape[0], c.shape[1]
        step = bs // chunks
        for j in range(chunks):
            bsl = pl.ds(j * step, step)
            copies.append(pltpu.make_async_copy(c.at[bsl, pl.ds(1, w - 1)], o.at[bsl, pl.ds(0, w - 1)],
                                                sem.at[len(copies)]))
        copies.append(pltpu.make_async_copy(nrow, o.at[pl.ds(0, bs), pl.ds(w - 1, 1)], sem.at[len(copies)]))
    for cp in copies:
        cp.start()
    for cp in copies:
        cp.wait()


def _cache_shift(caches, kvn):
    bs = caches[0].shape[0]
    chunks = 4 if bs % 4 == 0 else 1
    c4 = [c.reshape(bs, c.shape[1], ROW_TILES, LANES) for c in caches]
    n4 = [a.reshape(bs, 1, ROW_TILES, LANES) for a in kvn]
    any_spec = pl.BlockSpec(memory_space=pl.ANY)
    outs = pl.pallas_call(
        functools.partial(_cache_shift_kernel, chunks=chunks),
        in_specs=[any_spec] * 6,
        out_specs=[any_spec] * 3,
        out_shape=[jax.ShapeDtypeStruct(c.shape, F32) for c in c4],
        scratch_shapes=[pltpu.SemaphoreType.DMA((3 * (chunks + 1),))],
        name="cache_shift",
    )(*c4, *n4)
    return outs


def kernel(x_prompt, x_sample, cache_kv_g0, cache_kv_g1, cache_kv_g2, state_conv, state_h, p_prompt, p_sample, g_mix, w_in, conv_w, conv_b, w_rg_a, b_rg_a, w_rg_x, b_rg_x, lru_lambda, g_q, g_k, w_br_lru, w_br_att, w_out, g_ffn, w_router, b_router, w_gu, b_gu, w_down, b_down, g_ple, w_ple_gate, w_ple):
    depth = g_mix.shape[0]
    assert depth == 1
    bsz, seq, _ = x_prompt.shape
    bs, dec_seq, _ = x_sample.shape
    assert dec_seq == 1 and seq % TM == 0
    l = 0
    row = lambda v: v.reshape(1, -1).astype(F32)
    w_in_l = w_in[l]
    wts = dict(
        g_mix=row(g_mix[l]),
        w_in_a=jnp.concatenate([w_in_l[:, OFF_XR:OFF_Q], w_in_l[:, OFF_GLRU:OFF_GATT]], axis=1).astype(BF16),
        w_in_b=jnp.concatenate([w_in_l[:, OFF_Q:OFF_GLRU], w_in_l[:, OFF_GATT:N_IN]], axis=1).astype(BF16),
        conv_w=conv_w[l].astype(F32), conv_b=row(conv_b[l]),
        w_rg_a=w_rg_a[l].astype(BF16), b_rg_a=row(b_rg_a[l]),
        w_rg_x=w_rg_x[l].astype(BF16), b_rg_x=row(b_rg_x[l]),
        lru_lambda=row(lru_lambda[l]), g_q=row(g_q[l]), g_k=row(g_k[l]),
        w_br_lru=w_br_lru[l].astype(BF16), w_br_att=w_br_att[l].astype(BF16), w_out=w_out[l].astype(BF16),
        g_ffn=row(g_ffn[l]), w_router=w_router[l].astype(F32), b_router=row(b_router[l]),
        w_gu=w_gu[l].astype(BF16), b_gu=b_gu[l].reshape(N_EXPERTS, 1, 2 * D_FF).astype(F32),
        w_down=w_down[l].astype(BF16), b_down=b_down[l].reshape(N_EXPERTS, 1, D_MODEL).astype(F32),
        g_ple=row(g_ple[l]), w_ple_gate=w_ple_gate[l].astype(BF16), w_ple=w_ple[l].astype(BF16),
    )

    n = bsz * seq
    xp = x_prompt.reshape(n, D_MODEL)
    rope_p = _rope_tables(jnp.arange(seq, dtype=jnp.int32))
    ma, conv_p, h_p = _branch_a(xp, bsz, seq, wts['g_mix'], wts['w_in_a'], wts['conv_w'], wts['conv_b'],
                                wts['w_rg_a'], wts['b_rg_a'], wts['w_rg_x'], wts['b_rg_x'], wts['lru_lambda'],
                                wts['w_br_lru'])
    qkv = _qkv(xp, bsz, seq, wts['g_mix'], wts['w_in_b'], wts['g_q'], wts['g_k'], rope_p)
    qs, ks, vs = qkv[0:3], qkv[3:6], qkv[6:9]
    sg, kv2_p, kv1_p, kv0_p = qkv[9:13]
    os_, lses = [], []
    for g, (_, dil) in enumerate(GROUPS):
        o, lse = _attention_group(qs[g], ks[g], vs[g], bsz, seq, dil)
        os_.append(o)
        lses.append(lse)
    x1, hf_rows, top_e, gates, rank, counts = _merge_route(
        xp, ma, sg, os_ + lses, wts['w_br_att'], wts['w_out'], wts['g_ffn'], wts['w_router'], wts['b_router'],
        TM, True, "merge_prompt")
    y_p = _moe_and_ple(x1, hf_rows, top_e, gates, rank, counts, p_prompt[l].reshape(n, PLE_DIM), wts,
                       bm=512, tm=TM, tag="prompt")

    xs = x_sample.reshape(bs, D_MODEL)
    rope_s = _rope_tables(PAST_LEN + jnp.arange(1, dtype=jnp.int32))
    caches = [c[l].reshape(bs, c.shape[2], 2 * GROUP_W) for c in (cache_kv_g0, cache_kv_g1, cache_kv_g2)]
    (ma_s, sg_s, q_s, kvn0, kvn1, kvn2, conv_s, h_s) = _sample_in(
        xs, state_conv[l].reshape(bs, (CONV_W - 1) * D_RNN), state_h[l], rope_s, wts)
    kvn = (kvn0, kvn1, kvn2)
    att_s = _sample_attention(q_s, kvn, caches)
    new_caches = _cache_shift(caches, kvn)
    x1_s, hf_s, top_e_s, gates_s, rank_s, counts_s = _merge_route(
        xs, ma_s, sg_s, [att_s], wts['w_br_att'], wts['w_out'], wts['g_ffn'], wts['w_router'], wts['b_router'],
        bs, False, "merge_sample")
    y_s = _moe_and_ple(x1_s, hf_s, top_e_s, gates_s, rank_s, counts_s, p_sample[l].reshape(bs, PLE_DIM), wts,
                       bm=128, tm=bs, tag="sample")

    kv_shape = lambda t, w: t.reshape(1, t.shape[0], w, 2, HEADS_PER_GROUP, HEAD_DIM)
    return (
        y_p.reshape(bsz, seq, D_MODEL),
        y_s.reshape(bs, 1, D_MODEL),
        kv_shape(kv0_p, GROUPS[0][0]), kv_shape(kv1_p, GROUPS[1][0]), kv_shape(kv2_p, GROUPS[2][0]),
        conv_p.reshape(1, bsz, CONV_W - 1, D_RNN), h_p.reshape(1, bsz, D_RNN),
        kv_shape(new_caches[0], GROUPS[0][0]), kv_shape(new_caches[1], GROUPS[1][0]),
        kv_shape(new_caches[2], GROUPS[2][0]),
        conv_s.reshape(1, bs, CONV_W - 1, D_RNN), h_s.reshape(1, bs, D_RNN),
    )
```
